```python
import jax, jax.numpy as jnp
from jax import lax
import numpy as np

D_MODEL = 2048
BATCH = 4
SEQ = 2048
DEPTH = 2
DEC_BATCH = 32
DEC_SEQ = 4
PAST_LEN = 8192
PAGE_SIZE = 128

HEAD_DIM = 128
N_ATTN_HEADS = D_MODEL // 2 // HEAD_DIM
D_ATTN = N_ATTN_HEADS * HEAD_DIM
D_CONV = D_MODEL // 4
N_MLSTM_HEADS = D_MODEL // 4 // HEAD_DIM
D_MLSTM = N_MLSTM_HEADS * HEAD_DIM
D_MIX = D_ATTN + D_CONV + D_MLSTM
D_FF = 4 * D_MODEL
ROT_DIM = HEAD_DIM // 4
ROPE_THETA = 500000.0
MOBA_BLOCK = 256
MOBA_TOPK = 3
MOBA_Q_CHUNK = 32
CONV_WIDTH = 31
MLSTM_CHUNK = 64
RMS_EPS = 1e-6
LN_EPS = 1e-5
SPLIT_SIZES = (D_ATTN, D_ATTN, D_ATTN, D_CONV, D_CONV, D_MLSTM, D_MLSTM, D_MLSTM, D_MLSTM, N_MLSTM_HEADS, N_MLSTM_HEADS)
SPLIT_POINTS = tuple(int(s) for s in np.cumsum(SPLIT_SIZES)[:-1])
D_IN = sum(SPLIT_SIZES)

kernel_name = 'hybrid_moba_conv_mlstm_step'


def rmsnorm(x, g):
    xf = x.astype(jnp.float32)
    y = xf * lax.rsqrt(jnp.mean(xf * xf, axis=-1, keepdims=True) + RMS_EPS)
    return (y * g.astype(jnp.float32)).astype(x.dtype)


def layernorm(x, g):
    xf = x.astype(jnp.float32)
    mu = jnp.mean(xf, axis=-1, keepdims=True)
    var = jnp.mean(jnp.square(xf - mu), axis=-1, keepdims=True)
    return ((xf - mu) * lax.rsqrt(var + LN_EPS) * g.astype(jnp.float32)).astype(x.dtype)


def rope(x, pos):
    half = ROT_DIM // 2
    inv_freq = ROPE_THETA ** (-jnp.arange(half, dtype=jnp.float32) / half)
    ang = pos.astype(jnp.float32)[:, None] * inv_freq[None, :]
    cos = jnp.cos(ang)[None, :, None, :]
    sin = jnp.sin(ang)[None, :, None, :]
    x1 = x[..., :half].astype(jnp.float32)
    x2 = x[..., half:ROT_DIM].astype(jnp.float32)
    rot = jnp.concatenate([x1 * cos - x2 * sin, x2 * cos + x1 * sin], axis=-1)
    return jnp.concatenate([rot.astype(x.dtype), x[..., ROT_DIM:]], axis=-1)


def to_blocks(parts):
    total = sum(p.shape[1] for p in parts)
    nb = max(-(-total // MOBA_BLOCK), MOBA_TOPK)
    b, _, h, d = parts[-1].shape
    pad = jnp.zeros((b, nb * MOBA_BLOCK - total, h, d), parts[-1].dtype)
    return jnp.concatenate(list(parts) + [pad], axis=1).reshape(b, nb, MOBA_BLOCK, h, d)


def moba_chunk(q, pos, own, kb, vb, kmean):
    b, nb = kb.shape[:2]
    h = q.shape[2]
    scale = HEAD_DIM ** -0.5
    gate = jnp.einsum('bqhd,bnhd->bhqn', q.astype(jnp.float32), kmean)
    past = jnp.arange(nb) < own
    gate = jnp.where(past, gate, -jnp.inf)
    _, sel = lax.top_k(gate, MOBA_TOPK)
    valid = jnp.arange(MOBA_TOPK) < own
    bi = jnp.arange(b)[:, None, None, None]
    hi = jnp.arange(h)[None, :, None, None]
    ksel = kb[bi, sel, :, hi]
    vsel = vb[bi, sel, :, hi]
    s_sel = jnp.einsum('bqhd,bhqjkd->bhqjk', q, ksel).astype(jnp.float32) * scale
    s_sel = jnp.where(valid[:, None], s_sel, -jnp.inf).reshape(b, h, q.shape[1], MOBA_TOPK * MOBA_BLOCK)
    kown = lax.dynamic_index_in_dim(kb, own, axis=1, keepdims=False)
    vown = lax.dynamic_index_in_dim(vb, own, axis=1, keepdims=False)
    s_own = jnp.einsum('bqhd,bkhd->bhqk', q, kown).astype(jnp.float32) * scale
    kpos = own * MOBA_BLOCK + jnp.arange(MOBA_BLOCK)
    s_own = jnp.where(kpos[None, :] <= pos[:, None], s_own, -jnp.inf)
    p = jax.nn.softmax(jnp.concatenate([s_sel, s_own], axis=-1), axis=-1).astype(vb.dtype)
    p_sel = p[..., :MOBA_TOPK * MOBA_BLOCK].reshape(b, h, q.shape[1], MOBA_TOPK, MOBA_BLOCK)
    p_own = p[..., MOBA_TOPK * MOBA_BLOCK:]
    return (jnp.einsum('bhqjk,bhqjkd->bqhd', p_sel, vsel)
            + jnp.einsum('bhqk,bkhd->bqhd', p_own, vown))


def moba_attention(q, k_parts, v_parts, q_start, q_chunk):
    b, tq, h, d = q.shape
    kb = to_blocks(k_parts)
    vb = to_blocks(v_parts)
    kmean = jnp.mean(kb.astype(jnp.float32), axis=2)
    nq = tq // q_chunk
    qs = jnp.swapaxes(q.reshape(b, nq, q_chunk, h, d), 0, 1)

    def one(args):
        c, qc = args
        start = q_start + c * q_chunk
        pos = start + jnp.arange(q_chunk, dtype=jnp.int32)
        return moba_chunk(qc, pos, start // MOBA_BLOCK, kb, vb, kmean)

    out = lax.map(one, (jnp.arange(nq, dtype=jnp.int32), qs))
    return jnp.swapaxes(out, 0, 1).reshape(b, tq, h, d)


def conv_module(a, g, buf, w, bias, ln_g, ln_b):
    u = a * jax.nn.sigmoid(g)
    full = jnp.concatenate([buf.astype(u.dtype), u], axis=1)
    y = lax.conv_general_dilated(full, w[:, None, :], window_strides=(1,), padding='VALID',
                                 dimension_numbers=('NWC', 'WIO', 'NWC'),
                                 feature_group_count=D_CONV) + bias
    y = jax.nn.silu(layernorm(y, ln_g) + ln_b)
    return y, full[:, -(CONV_WIDTH - 1):]


def mlstm_chunked(q, k, v, ig, fg, c0, n0, m0, chunk):
    b, t, nh, d = q.shape
    nc = t // chunk
    f32 = jnp.float32

    def chunks(x):
        return jnp.moveaxis(x.astype(f32).reshape((b, nc, chunk) + x.shape[2:]), 1, 0)

    logf = jax.nn.log_sigmoid(fg.astype(f32))
    causal = jnp.tril(jnp.ones((chunk, chunk), bool))[None, :, :, None]

    def step(carry, xs):
        c, n, m = carry
        qc, kc, vc, ic, lfc = xs
        bcum = jnp.cumsum(lfc, axis=1)
        dmat = bcum[:, :, None, :] - bcum[:, None, :, :] + ic[:, None, :, :]
        dmat = jnp.where(causal, dmat, -jnp.inf)
        inter = bcum + m[:, None, :]
        mt = jnp.maximum(inter, jnp.max(dmat, axis=2))
        wd = jnp.exp(dmat - mt[:, :, None, :])
        wi = jnp.exp(inter - mt)
        s = jnp.einsum('bthd,bshd->btsh', qc, kc) * wd
        num = jnp.einsum('btsh,bshd->bthd', s, vc) + wi[..., None] * jnp.einsum('bhvk,bthk->bthv', c, qc)
        den = jnp.sum(s, axis=2) + wi * jnp.einsum('bhk,bthk->bth', n, qc)
        hc = num / jnp.maximum(jnp.abs(den), jnp.exp(-mt))[..., None]
        blast = bcum[:, -1]
        src = blast[:, None, :] - bcum + ic
        m_new = jnp.maximum(blast + m, jnp.max(src, axis=1))
        ws = jnp.exp(src - m_new[:, None, :])
        wc = jnp.exp(blast + m - m_new)
        c_new = wc[..., None, None] * c + jnp.einsum('bsh,bshv,bshk->bhvk', ws, vc, kc)
        n_new = wc[..., None] * n + jnp.einsum('bsh,bshk->bhk', ws, kc)
        return (c_new, n_new, m_new), hc

    carry0 = (c0.astype(f32), n0.astype(f32), m0.astype(f32))
    (c1, n1, m1), hs = lax.scan(step, carry0, (chunks(q), chunks(k), chunks(v), chunks(ig), chunks(logf)))
    h = jnp.moveaxis(hs, 0, 1).reshape(b, t, nh, d).astype(q.dtype)
    return h, (c1, n1, m1)


def mixer(z, pos, k_prefix, v_prefix, q_start, q_chunk, conv_buf, c0, n0, m0, m_chunk,
          w_in, conv_w, conv_b, conv_ln_g, conv_ln_b, b_i, b_f, m_norm_w):
    b, t, _ = z.shape
    proj = jnp.einsum('btd,dn->btn', z, w_in)
    qa, ka, va, ca, cg, qm, km, vm, om, im, fm = jnp.split(proj, SPLIT_POINTS, axis=-1)

    def heads(u, nh):
        return u.reshape(b, t, nh, HEAD_DIM)

    qa = rope(heads(qa, N_ATTN_HEADS), pos)
    ka = rope(heads(ka, N_ATTN_HEADS), pos)
    va = heads(va, N_ATTN_HEADS)
    att = moba_attention(qa, k_prefix + [ka], v_prefix + [va], q_start, q_chunk)
    cv, new_buf = conv_module(ca, cg, conv_buf, conv_w, conv_b, conv_ln_g, conv_ln_b)
    hm, (c1, n1, m1) = mlstm_chunked(heads(qm, N_MLSTM_HEADS), heads(km, N_MLSTM_HEADS) * (HEAD_DIM ** -0.5),
                                     heads(vm, N_MLSTM_HEADS), im + b_i, fm + b_f, c0, n0, m0, m_chunk)
    hm = layernorm(hm, m_norm_w.reshape(N_MLSTM_HEADS, HEAD_DIM)) * jax.nn.sigmoid(heads(om, N_MLSTM_HEADS))
    mixed = jnp.concatenate([att.reshape(b, t, D_ATTN), cv, hm.reshape(b, t, D_MLSTM)], axis=-1)
    return mixed, (ka, va, new_buf, c1, n1, m1)


def ffn(z, w_up, w_down):
    hid = jnp.square(jax.nn.relu(jnp.einsum('btd,df->btf', z, w_up)))
    return jnp.einsum('btf,fd->btd', hid, w_down)


def setup_inputs(seed: int = 0) -> dict:
    key = jax.random.key(seed)
    ks = jax.random.split(key, 24)
    f32 = jnp.float32
    n_pages = PAST_LEN // PAGE_SIZE
    n_used = DEC_BATCH * n_pages
    n_pool = n_used + max(1, n_used // 4)

    def nrm(k, shape, scale):
        return jax.random.normal(k, shape, f32) * scale

    return {
        'x_prompt': nrm(ks[0], (BATCH, SEQ, D_MODEL), 1.0),
        'x_sample': nrm(ks[1], (DEC_BATCH, DEC_SEQ, D_MODEL), 1.0),
        'cache_k': nrm(ks[2], (DEPTH, n_pool, PAGE_SIZE, N_ATTN_HEADS, HEAD_DIM), 1.0),
        'cache_v': nrm(ks[3], (DEPTH, n_pool, PAGE_SIZE, N_ATTN_HEADS, HEAD_DIM), 1.0),
        'page_table': jax.random.permutation(ks[4], n_pool)[:n_used].reshape(DEC_BATCH, n_pages).astype(jnp.int32),
        'state_conv': nrm(ks[5], (DEPTH, DEC_BATCH, CONV_WIDTH - 1, D_CONV), 0.5),
        'state_mlstm_c': nrm(ks[6], (DEPTH, DEC_BATCH, N_MLSTM_HEADS, HEAD_DIM, HEAD_DIM), 0.05),
        'state_mlstm_n': nrm(ks[7], (DEPTH, DEC_BATCH, N_MLSTM_HEADS, HEAD_DIM), 0.1),
        'state_mlstm_m': nrm(ks[8], (DEPTH, DEC_BATCH, N_MLSTM_HEADS), 1.0),
        'norm_mix': 1.0 + nrm(ks[9], (DEPTH, D_MODEL), 0.02),
        'w_in': nrm(ks[10], (DEPTH, D_MODEL, D_IN), D_MODEL ** -0.5),
        'conv_w': nrm(ks[11], (DEPTH, CONV_WIDTH, D_CONV), CONV_WIDTH ** -0.5),
        'conv_b': nrm(ks[12], (DEPTH, D_CONV), 0.02),
        'conv_ln_g': 1.0 + nrm(ks[13], (DEPTH, D_CONV), 0.02),
        'conv_ln_b': nrm(ks[14], (DEPTH, D_CONV), 0.02),
        'mlstm_b_i': nrm(ks[15], (DEPTH, N_MLSTM_HEADS), 0.1),
        'mlstm_b_f': 3.0 + nrm(ks[16], (DEPTH, N_MLSTM_HEADS), 0.5),
        'mlstm_norm_w': 1.0 + nrm(ks[17], (DEPTH, D_MLSTM), 0.02),
        'w_out': nrm(ks[18], (DEPTH, D_MIX, D_MODEL), D_MIX ** -0.5),
        'norm_ffn': 1.0 + nrm(ks[19], (DEPTH, D_MODEL), 0.02),
        'w_up': nrm(ks[20], (DEPTH, D_MODEL, D_FF), D_MODEL ** -0.5),
        'w_down': nrm(ks[21], (DEPTH, D_FF, D_MODEL), D_FF ** -0.5),
        'norm_final': 1.0 + nrm(ks[22], (D_MODEL,), 0.02),
    }


def reference(x_prompt, x_sample, cache_k, cache_v, page_table, state_conv, state_mlstm_c, state_mlstm_n,
              state_mlstm_m, norm_mix, w_in, conv_w, conv_b, conv_ln_g, conv_ln_b, mlstm_b_i, mlstm_b_f,
              mlstm_norm_w, w_out, norm_ffn, w_up, w_down, norm_final):
    bp, tp = x_prompt.shape[:2]
    bs, ts = x_sample.shape[:2]
    pos_p = jnp.arange(tp, dtype=jnp.int32)
    pos_s = PAST_LEN + jnp.arange(ts, dtype=jnp.int32)
    zero_buf = jnp.zeros((bp, CONV_WIDTH - 1, D_CONV), x_prompt.dtype)
    zero_c = jnp.zeros((bp, N_MLSTM_HEADS, HEAD_DIM, HEAD_DIM), jnp.float32)
    zero_n = jnp.zeros((bp, N_MLSTM_HEADS, HEAD_DIM), jnp.float32)
    zero_m = jnp.zeros((bp, N_MLSTM_HEADS), jnp.float32)
    hp, hs = x_prompt, x_sample
    kp_l, vp_l, ks_l, vs_l, bufp_l, bufs_l = [], [], [], [], [], []
    cp_l, cs_l, np_l, ns_l, mp_l, ms_l = [], [], [], [], [], []
    for l in range(DEPTH):
        lw = (w_in[l], conv_w[l], conv_b[l], conv_ln_g[l], conv_ln_b[l], mlstm_b_i[l], mlstm_b_f[l], mlstm_norm_w[l])
        mix_p, (k1, v1, b1, c1, n1, m1) = mixer(rmsnorm(hp, norm_mix[l]), pos_p, [], [], 0, MOBA_Q_CHUNK,
                                                zero_buf, zero_c, zero_n, zero_m, MLSTM_CHUNK, *lw)
        hp = hp + jnp.einsum('btm,md->btd', mix_p, w_out[l])
        hp = hp + ffn(rmsnorm(hp, norm_ffn[l]), w_up[l], w_down[l])
        k_past = cache_k[l][page_table].reshape(bs, -1, N_ATTN_HEADS, HEAD_DIM)
        v_past = cache_v[l][page_table].reshape(bs, -1, N_ATTN_HEADS, HEAD_DIM)
        mix_s, (k2, v2, b2, c2, n2, m2) = mixer(rmsnorm(hs, norm_mix[l]), pos_s, [k_past], [v_past], PAST_LEN, ts,
                                                state_conv[l], state_mlstm_c[l], state_mlstm_n[l],
                                                state_mlstm_m[l], ts, *lw)
        hs = hs + jnp.einsum('btm,md->btd', mix_s, w_out[l])
        hs = hs + ffn(rmsnorm(hs, norm_ffn[l]), w_up[l], w_down[l])
        kp_l.append(k1); vp_l.append(v1); bufp_l.append(b1); cp_l.append(c1); np_l.append(n1); mp_l.append(m1)
        ks_l.append(k2); vs_l.append(v2); bufs_l.append(b2); cs_l.append(c2); ns_l.append(n2); ms_l.append(m2)
    y_prompt = rmsnorm(hp, norm_final)
    y_sample = rmsnorm(hs, norm_final)
    return (y_prompt, y_sample,
            jnp.stack(kp_l), jnp.stack(vp_l), jnp.stack(ks_l), jnp.stack(vs_l),
            jnp.stack(bufp_l), jnp.stack(bufs_l),
            jnp.stack(cp_l), jnp.stack(cs_l), jnp.stack(np_l), jnp.stack(ns_l),
            jnp.stack(mp_l), jnp.stack(ms_l))
```

```python
import functools

import jax
import jax.numpy as jnp
import numpy as np
from jax import lax
from jax.experimental import pallas as pl
from jax.experimental.pallas import tpu as pltpu

F32 = jnp.float32
BF16 = jnp.bfloat16
NEG_INF = float("-inf")

HEAD_DIM = 128
ROT_DIM = HEAD_DIM // 4
ROPE_THETA = 500000.0
MOBA_BLOCK = 256
MOBA_TOPK = 3
MOBA_Q_CHUNK = 32
CONV_WIDTH = 31
MLSTM_CHUNK = 64
RMS_EPS = 1e-6
LN_EPS = 1e-5

LANES = 128
SUBLANES = 8
VMEM_LIMIT_BYTES = 52 * 1024 * 1024

CONV_HALO = 32
SAMPLE_PAD = SUBLANES
PAGES_PER_CHUNK = 8


def _params(*sem):
    return pltpu.CompilerParams(dimension_semantics=sem, vmem_limit_bytes=VMEM_LIMIT_BYTES)


def _dot(a, b):
    return jnp.dot(a, b, preferred_element_type=F32)


def _dot_nt(a, b):
    return lax.dot_general(a, b, (((1,), (1,)), ((), ())), preferred_element_type=F32)


def _split2(x):
    hi = x.astype(BF16)
    lo = (x - hi.astype(F32)).astype(BF16)
    return hi, lo


def _dot_nt_precise(a, b):
    ah, al = _split2(a)
    bh, bl = _split2(b)
    return _dot_nt(ah, bh) + (_dot_nt(ah, bl) + _dot_nt(al, bh))


def _act_dtype(tile_rows):
    return BF16 if tile_rows % (2 * SUBLANES) == 0 else F32


def _pad_rows(x, rows):
    if x.shape[0] == rows:
        return x
    return jnp.concatenate([x, jnp.zeros((rows - x.shape[0], x.shape[1]), x.dtype)], axis=0)


def _topk_select(gate, n_cand, own):
    lane = lax.broadcasted_iota(jnp.int32, gate.shape, 1)
    rank = jnp.zeros(gate.shape, jnp.int32)
    for m in range(n_cand):
        gm = gate[:, m:m + 1]
        beats = (gm > gate) | ((gm == gate) & (m < lane))
        hit = jnp.where(beats, 1, 0)
        rank = rank + (hit if own is None else jnp.where(m < own, hit, 0))
    limit = n_cand if own is None else own
    return jnp.where((lane < limit) & (rank < MOBA_TOPK), 1.0, 0.0)


def _inproj_kernel(x_ref, g_ref, w_ref, wg_ref, ca_ref, cm_ref, cp_ref, proj_ref, gates_ref, z_ref,
                   *, n_rope_tiles):
    j = pl.program_id(1)

    @pl.when(j == 0)
    def _():
        x = x_ref[...]
        z = x * lax.rsqrt(jnp.mean(x * x, axis=-1, keepdims=True) + RMS_EPS) * g_ref[...]
        zb = z.astype(BF16)
        z_ref[...] = zb
        gates_ref[...] = _dot(zb, wg_ref[...])

    acc = _dot(z_ref[...], w_ref[...])

    @pl.when(j < n_rope_tiles)
    def _():
        ca, cm, cp = ca_ref[...], cm_ref[...], cp_ref[...]
        for h in range(acc.shape[1] // HEAD_DIM):
            c = acc[:, h * HEAD_DIM:(h + 1) * HEAD_DIM]
            rot = (c * ca + pltpu.roll(c, HEAD_DIM - ROT_DIM // 2, 1) * cm
                   + pltpu.roll(c, ROT_DIM // 2, 1) * cp)
            proj_ref[:, h * HEAD_DIM:(h + 1) * HEAD_DIM] = rot

    @pl.when(j >= n_rope_tiles)
    def _():
        proj_ref[...] = acc


def _rope_tables(pos):
    half = ROT_DIM // 2
    inv_freq = ROPE_THETA ** (-jnp.arange(half, dtype=F32) / half)
    ang = pos.astype(F32)[:, None] * inv_freq[None, :]
    cos, sin = jnp.cos(ang), jnp.sin(ang)
    t = pos.shape[0]
    rest = jnp.zeros((t, HEAD_DIM - ROT_DIM), F32)
    zero = jnp.zeros((t, half), F32)
    ca = jnp.concatenate([cos, cos, rest + 1.0], axis=1)
    cm = jnp.concatenate([-sin, zero, rest], axis=1)
    cp = jnp.concatenate([zero, sin, rest], axis=1)
    return ca, cm, cp


def _inproj(x, g, w_main, w_gates, tables, *, tm, tn, d_qk):
    m, d = x.shape
    n = w_main.shape[1]
    t_rows = tables[0].shape[0]
    n_row_tiles = t_rows // tm
    tab_spec = pl.BlockSpec((tm, LANES), lambda i, j: (i % n_row_tiles, 0))
    return pl.pallas_call(
        functools.partial(_inproj_kernel, n_rope_tiles=d_qk // tn),
        grid=(m // tm, n // tn),
        in_specs=[
            pl.BlockSpec((tm, d), lambda i, j: (i, 0)),
            pl.BlockSpec((1, d), lambda i, j: (0, 0)),
            pl.BlockSpec((d, tn), lambda i, j: (0, j)),
            pl.BlockSpec((d, 2 * LANES), lambda i, j: (0, 0)),
            tab_spec, tab_spec, tab_spec,
        ],
        out_specs=[
            pl.BlockSpec((tm, tn), lambda i, j: (i, j)),
            pl.BlockSpec((tm, 2 * LANES), lambda i, j: (i, 0)),
        ],
        out_shape=[jax.ShapeDtypeStruct((m, n), F32), jax.ShapeDtypeStruct((m, 2 * LANES), F32)],
        scratch_shapes=[pltpu.VMEM((tm, d), BF16)],
        compiler_params=_params("parallel", "arbitrary"),
        name="inproj",
    )(x, g, w_main, w_gates, *tables)


def _moba_prompt_kernel(q_ref, k_ref, v_ref, o_ref, kb_ref, vb_ref, km_ref, s_ref, m_ref, l_ref, acc_ref,
                        *, nb):
    i = pl.program_id(2)
    blk = MOBA_BLOCK

    @pl.when(i == 0)
    def _():
        kb_ref[...] = k_ref[...].astype(BF16)
        vb_ref[...] = v_ref[...].astype(BF16)
        km_ref[...] = jnp.zeros_like(km_ref)
        for n in range(nb):
            km_ref[n:n + 1, :] = jnp.mean(k_ref[n * blk:(n + 1) * blk, :], axis=0, keepdims=True)

    q = q_ref[...]
    sel = _topk_select(_dot_nt_precise(q, km_ref[...]), nb - 1, i)
    qs = (q * (HEAD_DIM ** -0.5)).astype(BF16)
    row = lax.broadcasted_iota(jnp.int32, (blk, blk), 0)
    col = lax.broadcasted_iota(jnp.int32, (blk, blk), 1)

    m_ref[...] = jnp.full(m_ref.shape, NEG_INF, F32)

    def scores(n, mask):
        s = jnp.where(mask, _dot_nt(qs, kb_ref[n * blk:(n + 1) * blk, :]), NEG_INF)
        s_ref[:, n * blk:(n + 1) * blk] = s
        m_ref[...] = jnp.maximum(m_ref[...], jnp.max(s, axis=1, keepdims=True))

    for n in range(nb):
        if n < nb - 1:
            @pl.when(n < i)
            def _():
                scores(n, sel[:, n:n + 1] > 0.5)

        @pl.when(n == i)
        def _():
            scores(n, col <= row)

    l_ref[...] = jnp.zeros_like(l_ref)
    acc_ref[...] = jnp.zeros_like(acc_ref)
    mx = m_ref[:, 0:1]
    for n in range(nb):
        @pl.when(n <= i)
        def _():
            p = jnp.exp(s_ref[:, n * blk:(n + 1) * blk] - mx)
            l_ref[...] += jnp.sum(p, axis=1, keepdims=True)
            acc_ref[...] += _dot(p.astype(BF16), vb_ref[n * blk:(n + 1) * blk, :])

    o_ref[...] = (acc_ref[...] / l_ref[:, 0:1]).astype(o_ref.dtype)


def _moba_prompt(proj, *, batch, seq, n_heads):
    nb = seq // MOBA_BLOCK
    k_col, v_col = n_heads, 2 * n_heads
    return pl.pallas_call(
        functools.partial(_moba_prompt_kernel, nb=nb),
        grid=(batch, n_heads, nb),
        in_specs=[
            pl.BlockSpec((MOBA_BLOCK, HEAD_DIM), lambda b, h, i: (b * nb + i, h)),
            pl.BlockSpec((seq, HEAD_DIM), lambda b, h, i: (b, k_col + h)),
            pl.BlockSpec((seq, HEAD_DIM), lambda b, h, i: (b, v_col + h)),
        ],
        out_specs=pl.BlockSpec((MOBA_BLOCK, HEAD_DIM), lambda b, h, i: (b * nb + i, h)),
        out_shape=jax.ShapeDtypeStruct((batch * seq, n_heads * HEAD_DIM), BF16),
        scratch_shapes=[
            pltpu.VMEM((seq, HEAD_DIM), BF16),
            pltpu.VMEM((seq, HEAD_DIM), BF16),
            pltpu.VMEM((LANES, HEAD_DIM), F32),
            pltpu.VMEM((MOBA_BLOCK, seq), F32),
            pltpu.VMEM((MOBA_BLOCK, LANES), F32),
            pltpu.VMEM((MOBA_BLOCK, LANES), F32),
            pltpu.VMEM((MOBA_BLOCK, HEAD_DIM), F32),
        ],
        compiler_params=_params("parallel", "parallel", "arbitrary"),
        name="moba_prompt",
    )(proj, proj, proj)


def _moba_sample_kernel(pt_ref, q_ref, kn_ref, vn_ref, ck_ref, cv_ref, o_ref,
                        buf, sem, s_ref, p_ref, ksum_ref, acc_ref, qf_ref, qs_ref, l_ref,
                        *, layer, batch, n_pages, n_heads):
    ch = PAGES_PER_CHUNK
    n_chunks = n_pages // ch
    page = ck_ref.shape[2] // n_heads
    pages_per_block = MOBA_BLOCK // page
    n_blocks = n_pages // pages_per_block
    rows = n_heads * SAMPLE_PAD
    d_attn = n_heads * HEAD_DIM

    def start_chunk(src_ref, b, cc, slot):
        for j in range(ch):
            pid = pt_ref[b * n_pages + cc * ch + j]
            pltpu.make_async_copy(src_ref.at[layer, pid], buf.at[slot, j], sem.at[slot]).start()

    def wait_chunk(slot):
        for j in range(ch):
            pltpu.make_async_copy(ck_ref.at[layer, 0], buf.at[slot, j], sem.at[slot]).wait()

    def load_page(slot, j):
        return jnp.concatenate(
            [buf[slot, j, pl.ds(h, page, stride=n_heads), :] for h in range(n_heads)], axis=1)

    ksum_ref[...] = jnp.zeros_like(ksum_ref)
    start_chunk(ck_ref, 0, 0, 0)

    head_of_row = lax.broadcasted_iota(jnp.int32, (rows, d_attn), 0) >> (SAMPLE_PAD.bit_length() - 1)
    head_of_col = lax.broadcasted_iota(jnp.int32, (rows, d_attn), 1) >> (HEAD_DIM.bit_length() - 1)
    tok_of_row = lax.broadcasted_iota(jnp.int32, (rows, LANES), 0) & (SAMPLE_PAD - 1)
    lane = lax.broadcasted_iota(jnp.int32, (rows, LANES), 1)

    def per_batch(b, carry):
        r0 = pl.multiple_of(b * SAMPLE_PAD, SAMPLE_PAD)
        qb = q_ref[pl.ds(r0, SAMPLE_PAD), :]
        qbd = jnp.where(head_of_row == head_of_col, jnp.concatenate([qb] * n_heads, axis=0), 0.0)
        qf_ref[...] = qbd
        qs_ref[...] = (qbd * (HEAD_DIM ** -0.5)).astype(BF16)

        def k_chunk(cc, c2):
            slot = cc % 2

            @pl.when(cc + 1 < n_chunks)
            def _():
                start_chunk(ck_ref, b, cc + 1, 1 - slot)

            @pl.when(cc + 1 == n_chunks)
            def _():
                start_chunk(cv_ref, b, 0, 1 - slot)

            wait_chunk(slot)
            for jb in range(ch // pages_per_block):
                ksum = None
                for jp in range(pages_per_block):
                    j = jb * pages_per_block + jp
                    kp = load_page(slot, j)
                    cs = jnp.sum(kp, axis=0, keepdims=True)
                    ksum = cs if ksum is None else ksum + cs
                    off = pl.multiple_of((cc * ch + j) * page, page)
                    s_ref[:, pl.ds(off, page)] = _dot_nt(qs_ref[...], kp.astype(BF16))
                ksum_ref[pl.ds(cc * (ch // pages_per_block) + jb, 1), :] = ksum
            return c2

        lax.fori_loop(0, n_chunks, k_chunk, 0)

        kmean = ksum_ref[...] * (1.0 / MOBA_BLOCK)
        sel = _topk_select(_dot_nt_precise(qf_ref[...], kmean), n_blocks, None)

        kn = _pad_rows(kn_ref[pl.ds(r0, SAMPLE_PAD), :], LANES).astype(BF16)
        vn = _pad_rows(vn_ref[pl.ds(r0, SAMPLE_PAD), :], LANES).astype(BF16)
        s_own = jnp.where(lane <= tok_of_row, _dot_nt(qs_ref[...], kn), NEG_INF)
        mx = jnp.max(s_own, axis=1, keepdims=True)
        for n in range(n_blocks):
            sn = jnp.where(sel[:, n:n + 1] > 0.5, s_ref[:, n * MOBA_BLOCK:(n + 1) * MOBA_BLOCK], NEG_INF)
            s_ref[:, n * MOBA_BLOCK:(n + 1) * MOBA_BLOCK] = sn
            mx = jnp.maximum(mx, jnp.max(sn, axis=1, keepdims=True))
        p_own = jnp.exp(s_own - mx)
        lsum = jnp.sum(p_own, axis=1, keepdims=True)
        for n in range(n_blocks):
            p = jnp.exp(s_ref[:, n * MOBA_BLOCK:(n + 1) * MOBA_BLOCK] - mx)
            lsum = lsum + jnp.sum(p, axis=1, keepdims=True)
            p_ref[:, n * MOBA_BLOCK:(n + 1) * MOBA_BLOCK] = p.astype(BF16)
        l_ref[...] = jnp.broadcast_to(lsum, l_ref.shape)
        acc_ref[...] = _dot(p_own.astype(BF16), vn)

        def v_chunk(cc, c2):
            slot = cc % 2

            @pl.when(cc + 1 < n_chunks)
            def _():
                start_chunk(cv_ref, b, cc + 1, 1 - slot)

            @pl.when((cc + 1 == n_chunks) & (b + 1 < batch))
            def _():
                start_chunk(ck_ref, b + 1, 0, 1 - slot)

            wait_chunk(slot)
            for j in range(ch):
                off = pl.multiple_of((cc * ch + j) * page, page)
                acc_ref[...] += _dot(p_ref[:, pl.ds(off, page)], load_page(slot, j).astype(BF16))
            return c2

        lax.fori_loop(0, n_chunks, v_chunk, 0)

        inv_l = 1.0 / l_ref[:, 0:1]
        for h in range(n_heads):
            rs = slice(h * SAMPLE_PAD, (h + 1) * SAMPLE_PAD)
            cs = slice(h * HEAD_DIM, (h + 1) * HEAD_DIM)
            o_ref[pl.ds(r0, SAMPLE_PAD), cs] = (acc_ref[rs, cs] * inv_l[rs]).astype(o_ref.dtype)
        return carry

    lax.fori_loop(0, batch, per_batch, 0)


def _moba_sample(proj, cache_k, cache_v, page_table, *, layer, n_heads):
    batch, n_pages = page_table.shape
    m = proj.shape[0]
    d_attn = n_heads * HEAD_DIM
    page = cache_k.shape[2] // n_heads
    rows = n_heads * SAMPLE_PAD
    past = n_pages * page
    assert past % MOBA_BLOCK == 0 and MOBA_BLOCK % page == 0 and n_pages % PAGES_PER_CHUNK == 0
    assert (n_pages // PAGES_PER_CHUNK) % 2 == 0 and past // MOBA_BLOCK <= LANES
    grid_spec = pltpu.PrefetchScalarGridSpec(
        num_scalar_prefetch=1,
        grid=(1,),
        in_specs=[
            pl.BlockSpec((m, d_attn), lambda i, pt: (0, 0)),
            pl.BlockSpec((m, d_attn), lambda i, pt: (0, 1)),
            pl.BlockSpec((m, d_attn), lambda i, pt: (0, 2)),
            pl.BlockSpec(memory_space=pl.ANY),
            pl.BlockSpec(memory_space=pl.ANY),
        ],
        out_specs=pl.BlockSpec((m, d_attn), lambda i, pt: (0, 0)),
        scratch_shapes=[
            pltpu.VMEM((2, PAGES_PER_CHUNK, page * n_heads, HEAD_DIM), F32),
            pltpu.SemaphoreType.DMA((2,)),
            pltpu.VMEM((rows, past), F32),
            pltpu.VMEM((rows, past), BF16),
            pltpu.VMEM((LANES, d_attn), F32),
            pltpu.VMEM((rows, d_attn), F32),
            pltpu.VMEM((rows, d_attn), F32),
            pltpu.VMEM((rows, d_attn), BF16),
            pltpu.VMEM((rows, LANES), F32),
        ],
    )
    return pl.pallas_call(
        functools.partial(_moba_sample_kernel, layer=layer, batch=batch, n_pages=n_pages, n_heads=n_heads),
        grid_spec=grid_spec,
        out_shape=jax.ShapeDtypeStruct((m, d_attn), F32),
        compiler_params=_params("arbitrary"),
        name="moba_sample",
    )(page_table.reshape(-1), proj, proj, proj, cache_k, cache_v)


def _conv_kernel(a_ref, g_ref, st_ref, w_ref, b_ref, lng_ref, lnb_ref, y_ref, nb_ref, full_ref,
                 *, tt, sub, valid_last):
    t = pl.program_id(1)

    @pl.when(t == 0)
    def _():
        full_ref[0:CONV_HALO, :] = st_ref[0]

    @pl.when(t > 0)
    def _():
        full_ref[0:CONV_HALO, :] = full_ref[tt:tt + CONV_HALO, :]

    full_ref[CONV_HALO:CONV_HALO + tt, :] = a_ref[...] * jax.nn.sigmoid(g_ref[...])

    lead = CONV_HALO - (CONV_WIDTH - 1)
    for r0 in range(0, tt, sub):
        acc = jnp.broadcast_to(b_ref[...], (sub, b_ref.shape[1]))
        for j in range(CONV_WIDTH):
            acc = acc + w_ref[j:j + 1, :] * full_ref[lead + r0 + j:lead + r0 + j + sub, :]
        mu = jnp.mean(acc, axis=-1, keepdims=True)
        var = jnp.mean(jnp.square(acc - mu), axis=-1, keepdims=True)
        yn = (acc - mu) * lax.rsqrt(var + LN_EPS) * lng_ref[...] + lnb_ref[...]
        y_ref[r0:r0 + sub, :] = (yn * jax.nn.sigmoid(yn)).astype(y_ref.dtype)

    @pl.when(t == pl.num_programs(1) - 1)
    def _():
        nb_ref[0] = full_ref[lead + valid_last:CONV_HALO + valid_last, :]


def _conv(proj, state_pad, w, b, ln_g, ln_b, *, batch, tt, n_tiles, valid_last, a_col):
    d_conv = w.shape[1]
    sub = min(tt, 64)
    vec = pl.BlockSpec((1, d_conv), lambda bi, t: (0, 0))
    return pl.pallas_call(
        functools.partial(_conv_kernel, tt=tt, sub=sub, valid_last=valid_last),
        grid=(batch, n_tiles),
        in_specs=[
            pl.BlockSpec((tt, d_conv), lambda bi, t: (bi * n_tiles + t, a_col)),
            pl.BlockSpec((tt, d_conv), lambda bi, t: (bi * n_tiles + t, a_col + 1)),
            pl.BlockSpec((1, CONV_HALO, d_conv), lambda bi, t: (bi, 0, 0)),
            pl.BlockSpec((CONV_WIDTH, d_conv), lambda bi, t: (0, 0)),
            vec, vec, vec,
        ],
        out_specs=[
            pl.BlockSpec((tt, d_conv), lambda bi, t: (bi * n_tiles + t, 0)),
            pl.BlockSpec((1, CONV_WIDTH - 1, d_conv), lambda bi, t: (bi, 0, 0)),
        ],
        out_shape=[
            jax.ShapeDtypeStruct((batch * n_tiles * tt, d_conv), _act_dtype(tt)),
            jax.ShapeDtypeStruct((batch, CONV_WIDTH - 1, d_conv), F32),
        ],
        scratch_shapes=[pltpu.VMEM((CONV_HALO + tt, d_conv), F32)],
        compiler_params=_params("parallel", "arbitrary"),
        name="conv",
    )(proj, proj, state_pad, w, b, ln_g, ln_b)


def _split3(x):
    hi = x.astype(BF16)
    r1 = x - hi.astype(F32)
    mid = r1.astype(BF16)
    lo = (r1 - mid.astype(F32)).astype(BF16)
    return hi, mid, lo


def _mlstm_kernel(q_ref, k_ref, v_ref, og_ref, gt_ref, c0_ref, n0_ref, m0_ref, bias_ref, nw_ref,
                  h_ref, c1_ref, n1_ref, m1_ref, c_st, n_st, m_st, *, chunk, valid, n_heads):
    c = pl.program_id(1)

    @pl.when(c == 0)
    def _():
        c_st[...] = c0_ref[0]
        n_st[...] = n0_ref[0]
        m_st[...] = m0_ref[0]

    gt = gt_ref[...]
    ic = gt[:, :LANES] + bias_ref[0:1, :]
    fg = gt[:, LANES:] + bias_ref[1:2, :]
    lf = jnp.minimum(fg, 0.0) - jnp.log1p(jnp.exp(-jnp.abs(fg)))
    if valid < chunk:
        t_idx = lax.broadcasted_iota(jnp.int32, lf.shape, 0)
        lf = jnp.where(t_idx < valid, lf, 0.0)
        ic = jnp.where(t_idx < valid, ic, NEG_INF)

    r_i = lax.broadcasted_iota(jnp.int32, (LANES, LANES), 0)
    c_i = lax.broadcasted_iota(jnp.int32, (LANES, LANES), 1)
    tril = jnp.where(c_i <= r_i, 1.0, 0.0).astype(BF16)
    hi, mid, lo = _split3(_pad_rows(lf, LANES))
    bc = (_dot(tril, hi) + (_dot(tril, mid) + _dot(tril, lo)))[:chunk]

    m_row = m_st[...]
    r = ic - bc
    inter = bc + m_row
    blast = bc[chunk - 1:chunk, :]
    src = blast + r
    m_new = jnp.maximum(blast + m_row, jnp.max(src, axis=0, keepdims=True))
    ws = jnp.exp(src - m_new)
    wc = jnp.exp(blast + m_row - m_new)
    r_t = _pad_rows(r, LANES).T

    t_i = lax.broadcasted_iota(jnp.int32, (chunk, LANES), 0)
    s_i = lax.broadcasted_iota(jnp.int32, (chunk, LANES), 1)
    causal = s_i <= t_i

    for h in range(n_heads):
        hs = slice(h * HEAD_DIM, (h + 1) * HEAD_DIM)
        qh = q_ref[:, hs]
        kh = k_ref[:, hs] * (HEAD_DIM ** -0.5)
        vh = v_ref[:, hs]
        qb = qh.astype(BF16)
        kb = _pad_rows(kh, LANES).astype(BF16)
        vb = _pad_rows(vh, LANES).astype(BF16)

        dm = jnp.where(causal, bc[:, h:h + 1] + r_t[h:h + 1, :], NEG_INF)
        inter_h = inter[:, h:h + 1]
        mt = jnp.maximum(inter_h, jnp.max(dm, axis=1, keepdims=True))
        wd = jnp.exp(dm - mt)
        wi = jnp.exp(inter_h - mt)
        s = _dot_nt(qb, kb) * wd
        c_old = c_st[h]
        n_old = n_st[h:h + 1, :]
        num = _dot(s.astype(BF16), vb) + wi * _dot_nt(qb, c_old.astype(BF16))
        den = jnp.sum(s, axis=1, keepdims=True) + wi * jnp.sum(qh * n_old, axis=1, keepdims=True)
        hh = num / jnp.maximum(jnp.abs(den), jnp.exp(-mt))

        ws_h = ws[:, h:h + 1]
        wc_h = wc[:, h:h + 1]
        vw_t = _pad_rows(vh * ws_h, LANES).T.astype(BF16)
        c_st[h] = wc_h * c_old + _dot(vw_t, kb)
        n_st[h:h + 1, :] = wc_h * n_old + jnp.sum(kh * ws_h, axis=0, keepdims=True)

        mu = jnp.mean(hh, axis=-1, keepdims=True)
        var = jnp.mean(jnp.square(hh - mu), axis=-1, keepdims=True)
        y = (hh - mu) * lax.rsqrt(var + LN_EPS) * nw_ref[:, hs]
        h_ref[:, hs] = (y * jax.nn.sigmoid(og_ref[:, hs])).astype(h_ref.dtype)

    m_st[...] = m_new

    @pl.when(c == pl.num_programs(1) - 1)
    def _():
        c1_ref[0] = c_st[...]
        n1_ref[0] = n_st[...]
        m1_ref[0] = m_st[...]


def _mlstm(proj, gates, c0, n0, m0_row, bias_rows, norm_w, *, batch, chunk, n_chunks, valid, q_col):
    n_heads = c0.shape[1]
    d_m = n_heads * HEAD_DIM

    def col(k):
        return pl.BlockSpec((chunk, d_m), lambda b, c: (b * n_chunks + c, q_col + k))

    state_specs = [
        pl.BlockSpec((1, n_heads, HEAD_DIM, HEAD_DIM), lambda b, c: (b, 0, 0, 0)),
        pl.BlockSpec((1, n_heads, HEAD_DIM), lambda b, c: (b, 0, 0)),
        pl.BlockSpec((1, 1, LANES), lambda b, c: (b, 0, 0)),
    ]
    return pl.pallas_call(
        functools.partial(_mlstm_kernel, chunk=chunk, valid=valid, n_heads=n_heads),
        grid=(batch, n_chunks),
        in_specs=[
            col(0), col(1), col(2), col(3),
            pl.BlockSpec((chunk, 2 * LANES), lambda b, c: (b * n_chunks + c, 0)),
            *state_specs,
            pl.BlockSpec((2, LANES), lambda b, c: (0, 0)),
            pl.BlockSpec((1, d_m), lambda b, c: (0, 0)),
        ],
        out_specs=[pl.BlockSpec((chunk, d_m), lambda b, c: (b * n_chunks + c, 0)), *state_specs],
        out_shape=[
            jax.ShapeDtypeStruct((batch * n_chunks * chunk, d_m), _act_dtype(chunk)),
            jax.ShapeDtypeStruct((batch, n_heads, HEAD_DIM, HEAD_DIM), F32),
            jax.ShapeDtypeStruct((batch, n_heads, HEAD_DIM), F32),
            jax.ShapeDtypeStruct((batch, 1, LANES), F32),
        ],
        scratch_shapes=[
            pltpu.VMEM((n_heads, HEAD_DIM, HEAD_DIM), F32),
            pltpu.VMEM((n_heads, HEAD_DIM), F32),
            pltpu.VMEM((1, LANES), F32),
        ],
        compiler_params=_params("parallel", "arbitrary"),
        name="mlstm",
    )(proj, proj, proj, proj, gates, c0, n0, m0_row, bias_rows, norm_w)


def _outproj_kernel(x_ref, att_ref, cv_ref, hm_ref, wa_ref, wc_ref, wm_ref, o_ref):
    att, cv, hm = (r[...].astype(BF16) for r in (att_ref, cv_ref, hm_ref))
    o_ref[...] = x_ref[...] + (_dot(att, wa_ref[...]) + (_dot(cv, wc_ref[...]) + _dot(hm, wm_ref[...])))


def _outproj(x, att, cv, hm, wa, wc, wm, *, tm):
    m, d = x.shape

    def rows(width):
        return pl.BlockSpec((tm, width), lambda i: (i, 0))

    def whole(a):
        return pl.BlockSpec(a.shape, lambda i: (0, 0))

    return pl.pallas_call(
        _outproj_kernel,
        grid=(m // tm,),
        in_specs=[rows(d), rows(att.shape[1]), rows(cv.shape[1]), rows(hm.shape[1]),
                  whole(wa), whole(wc), whole(wm)],
        out_specs=rows(d),
        out_shape=jax.ShapeDtypeStruct((m, d), F32),
        compiler_params=_params("parallel"),
        name="outproj",
    )(x, att, cv, hm, wa, wc, wm)


def _ffn_kernel(x_ref, g_ref, wu_ref, wd_ref, o_ref, z_ref):
    f = pl.program_id(1)

    @pl.when(f == 0)
    def _():
        x = x_ref[...]
        z = x * lax.rsqrt(jnp.mean(x * x, axis=-1, keepdims=True) + RMS_EPS) * g_ref[...]
        z_ref[...] = z.astype(BF16)
        o_ref[...] = x

    hid = jnp.square(jnp.maximum(_dot(z_ref[...], wu_ref[...]), 0.0))
    o_ref[...] += _dot(hid.astype(BF16), wd_ref[...])


def _ffn(x, g, w_up, w_down, *, tm, tf):
    m, d = x.shape
    d_ff = w_up.shape[1]
    return pl.pallas_call(
        _ffn_kernel,
        grid=(m // tm, d_ff // tf),
        in_specs=[
            pl.BlockSpec((tm, d), lambda i, f: (i, 0)),
            pl.BlockSpec((1, d), lambda i, f: (0, 0)),
            pl.BlockSpec((d, tf), lambda i, f: (0, f)),
            pl.BlockSpec((tf, d), lambda i, f: (f, 0)),
        ],
        out_specs=pl.BlockSpec((tm, d), lambda i, f: (i, 0)),
        out_shape=jax.ShapeDtypeStruct((m, d), F32),
        scratch_shapes=[pltpu.VMEM((tm, d), BF16)],
        compiler_params=_params("parallel", "arbitrary"),
        name="ffn",
    )(x, g, w_up, w_down)


def _rmsnorm_kernel(x_ref, g_ref, o_ref):
    x = x_ref[...]
    o_ref[...] = x * lax.rsqrt(jnp.mean(x * x, axis=-1, keepdims=True) + RMS_EPS) * g_ref[...]


def _rmsnorm(x, g, *, tm):
    m, d = x.shape
    return pl.pallas_call(
        _rmsnorm_kernel,
        grid=(m // tm,),
        in_specs=[pl.BlockSpec((tm, d), lambda i: (i, 0)), pl.BlockSpec((1, d), lambda i: (0, 0))],
        out_specs=pl.BlockSpec((tm, d), lambda i: (i, 0)),
        out_shape=jax.ShapeDtypeStruct((m, d), F32),
        compiler_params=_params("parallel"),
        name="final_norm",
    )(x, g)


def _row_tile(m, cap):
    t = min(m, cap)
    while m % t:
        t //= 2
    return t


def kernel(x_prompt, x_sample, cache_k, cache_v, page_table, state_conv, state_mlstm_c, state_mlstm_n,
           state_mlstm_m, norm_mix, w_in, conv_w, conv_b, conv_ln_g, conv_ln_b, mlstm_b_i, mlstm_b_f,
           mlstm_norm_w, w_out, norm_ffn, w_up, w_down, norm_final):
    bp, tp, d_model = x_prompt.shape
    bs, ts, _ = x_sample.shape
    depth = w_in.shape[0]
    n_heads = cache_k.shape[3]
    d_attn = n_heads * HEAD_DIM
    d_conv = conv_w.shape[2]
    nh_m = mlstm_b_i.shape[1]
    d_m = nh_m * HEAD_DIM
    n_pages = page_table.shape[1]
    page = cache_k.shape[2]
    past_len = n_pages * page
    d_main = 3 * d_attn + 2 * d_conv + 4 * d_m
    assert w_in.shape[2] == d_main + 2 * nh_m and ts <= SAMPLE_PAD and tp % MOBA_BLOCK == 0
    assert tp % MLSTM_CHUNK == 0 and d_conv == d_m and d_attn % d_conv == 0

    mp = bp * tp
    ms = bs * SAMPLE_PAD
    hp = x_prompt.reshape(mp, d_model)
    hs = jnp.pad(x_sample, ((0, 0), (0, SAMPLE_PAD - ts), (0, 0))).reshape(ms, d_model)

    tab_p = _rope_tables(jnp.arange(tp, dtype=jnp.int32))
    tab_s = tuple(jnp.tile(t, (bs, 1)) for t in
                  _rope_tables(past_len + jnp.arange(SAMPLE_PAD, dtype=jnp.int32)))

    ck = cache_k.reshape(depth, cache_k.shape[1], page * n_heads, HEAD_DIM)
    cv = cache_v.reshape(depth, cache_v.shape[1], page * n_heads, HEAD_DIM)

    w_main = w_in[:, :, :d_main].astype(BF16)
    w_gates = jnp.zeros((depth, d_model, 2 * LANES), F32)
    w_gates = w_gates.at[:, :, :nh_m].set(w_in[:, :, d_main:d_main + nh_m])
    w_gates = w_gates.at[:, :, LANES:LANES + nh_m].set(w_in[:, :, d_main + nh_m:]).astype(BF16)
    w_out_b = w_out.astype(BF16)
    w_up_b = w_up.astype(BF16)
    w_down_b = w_down.astype(BF16)

    gate_bias = jnp.zeros((depth, 2, LANES), F32)
    gate_bias = gate_bias.at[:, 0, :nh_m].set(mlstm_b_i).at[:, 1, :nh_m].set(mlstm_b_f)

    conv_col = 3 * d_attn // d_conv
    mlstm_col = (3 * d_attn + 2 * d_conv) // d_m
    tm_p = _row_tile(mp, 512)
    tm_s = _row_tile(ms, 512)
    tn = 1024
    tt_p = MOBA_BLOCK

    zero_conv = jnp.zeros((bp, CONV_HALO, d_conv), F32)
    zero_c = jnp.zeros((bp, nh_m, HEAD_DIM, HEAD_DIM), F32)
    zero_n = jnp.zeros((bp, nh_m, HEAD_DIM), F32)
    zero_m = jnp.zeros((bp, 1, LANES), F32)
    lead = CONV_HALO - (CONV_WIDTH - 1)

    outs = {k: [] for k in ("kp", "vp", "ks", "vs", "bp", "bs", "cp", "cs", "np", "ns", "mp", "ms")}
    for l in range(depth):
        g_mix = norm_mix[l].reshape(1, d_model)
        g_ffn = norm_ffn[l].reshape(1, d_model)
        wa, wc, wm = w_out_b[l, :d_attn], w_out_b[l, d_attn:d_attn + d_conv], w_out_b[l, d_attn + d_conv:]
        cw, cb = conv_w[l], conv_b[l].reshape(1, d_conv)
        lng, lnb = conv_ln_g[l].reshape(1, d_conv), conv_ln_b[l].reshape(1, d_conv)
        nw = mlstm_norm_w[l].reshape(1, d_m)

        proj, gates = _inproj(hp, g_mix, w_main[l], w_gates[l], tab_p, tm=tm_p, tn=tn, d_qk=2 * d_attn)
        att = _moba_prompt(proj, batch=bp, seq=tp, n_heads=n_heads)
        cvp, bufp = _conv(proj, zero_conv, cw, cb, lng, lnb, batch=bp, tt=tt_p, n_tiles=tp // tt_p,
                          valid_last=tt_p, a_col=conv_col)
        hm, c1, n1, m1 = _mlstm(proj, gates, zero_c, zero_n, zero_m, gate_bias[l], nw, batch=bp,
                                chunk=MLSTM_CHUNK, n_chunks=tp // MLSTM_CHUNK, valid=MLSTM_CHUNK,
                                q_col=mlstm_col)
        hp = _outproj(hp, att, cvp, hm, wa, wc, wm, tm=tm_p)
        hp = _ffn(hp, g_ffn, w_up_b[l], w_down_b[l], tm=tm_p, tf=512)
        outs["kp"].append(proj[:, d_attn:2 * d_attn].reshape(bp, tp, n_heads, HEAD_DIM))
        outs["vp"].append(proj[:, 2 * d_attn:3 * d_attn].reshape(bp, tp, n_heads, HEAD_DIM))
        outs["bp"].append(bufp)
        outs["cp"].append(c1)
        outs["np"].append(n1)
        outs["mp"].append(m1[:, 0, :nh_m])

        proj, gates = _inproj(hs, g_mix, w_main[l], w_gates[l], tab_s, tm=tm_s, tn=tn, d_qk=2 * d_attn)
        att = _moba_sample(proj, ck, cv, page_table, layer=l, n_heads=n_heads)
        st = jnp.pad(state_conv[l], ((0, 0), (lead, 0), (0, 0)))
        cvs, bufs = _conv(proj, st, cw, cb, lng, lnb, batch=bs, tt=SAMPLE_PAD, n_tiles=1,
                          valid_last=ts, a_col=conv_col)
        m0 = jnp.pad(state_mlstm_m[l], ((0, 0), (0, LANES - nh_m))).reshape(bs, 1, LANES)
        hm, c1, n1, m1 = _mlstm(proj, gates, state_mlstm_c[l], state_mlstm_n[l], m0, gate_bias[l], nw,
                                batch=bs, chunk=SAMPLE_PAD, n_chunks=1, valid=ts, q_col=mlstm_col)
        hs = _outproj(hs, att, cvs, hm, wa, wc, wm, tm=tm_s)
        hs = _ffn(hs, g_ffn, w_up_b[l], w_down_b[l], tm=tm_s, tf=512)
        k_new = proj[:, d_attn:2 * d_attn].reshape(bs, SAMPLE_PAD, n_heads, HEAD_DIM)[:, :ts]
        v_new = proj[:, 2 * d_attn:3 * d_attn].reshape(bs, SAMPLE_PAD, n_heads, HEAD_DIM)[:, :ts]
        outs["ks"].append(k_new)
        outs["vs"].append(v_new)
        outs["bs"].append(bufs)
        outs["cs"].append(c1)
        outs["ns"].append(n1)
        outs["ms"].append(m1[:, 0, :nh_m])

    g_fin = norm_final.reshape(1, d_model)
    y_prompt = _rmsnorm(hp, g_fin, tm=tm_p).reshape(bp, tp, d_model)
    y_sample = _rmsnorm(hs, g_fin, tm=tm_s).reshape(bs, SAMPLE_PAD, d_model)[:, :ts]
    st = {k: jnp.stack(v) for k, v in outs.items()}
    return (y_prompt, y_sample, st["kp"], st["vp"], st["ks"], st["vs"], st["bp"], st["bs"],
            st["cp"], st["cs"], st["np"], st["ns"], st["mp"], st["ms"])
```
